```python
import math
import jax, jax.numpy as jnp
from jax import lax
import numpy as np

D_MODEL = 1024
BATCH = 4
SEQ = 8192
DEPTH = 4

N_MIXERS = 2
HEAD_DIM = 64
BLOCK = 128
A_HEADS = D_MODEL // HEAD_DIM
A_KV_HEADS = A_HEADS // 8
WINDOW = 128
N_BUCKETS = 32
MAX_DISTANCE = WINDOW
B_HEADS = D_MODEL // HEAD_DIM
N_EXPERTS = 32
TOP_K = 4
D_FF = D_MODEL
SWIGLU_LIMIT = 7.0
SWIGLU_ALPHA = 1.702
MOE_ROWS = 256
DN_ALPHA = (2 * DEPTH) ** 0.25
DN_BETA = (8 * DEPTH) ** -0.25
LN_EPS = 1e-5
N_A_LAYERS = (DEPTH + 1) // 2
N_B_LAYERS = DEPTH // 2

kernel_name = 'hybrid_swa_stickbreak_moe_deepnorm'


def layer_norm(x, g, b):
    xf = x.astype(jnp.float32)
    mu = jnp.mean(xf, axis=-1, keepdims=True)
    xc = xf - mu
    var = jnp.mean(xc * xc, axis=-1, keepdims=True)
    y = xc * lax.rsqrt(var + LN_EPS) * g.astype(jnp.float32) + b.astype(jnp.float32)
    return y.astype(x.dtype)


def t5_bucket(n):
    max_exact = N_BUCKETS // 2
    nf = jnp.maximum(n, 1).astype(jnp.float32)
    large = max_exact + (jnp.log(nf / max_exact) / math.log(MAX_DISTANCE / max_exact)
                         * (N_BUCKETS - max_exact)).astype(jnp.int32)
    large = jnp.minimum(large, N_BUCKETS - 1)
    return jnp.where(n < max_exact, n, large)


def sliding_window_attention(x, w_qkv, b_qkv, sinks, w_o, b_o, rel_bias):
    bsz, s, _ = x.shape
    nb = s // BLOCK
    grp = A_HEADS // A_KV_HEADS
    qkv = x @ w_qkv + b_qkv
    q, k, v = jnp.split(qkv, [A_HEADS * HEAD_DIM, (A_HEADS + A_KV_HEADS) * HEAD_DIM], axis=-1)
    q = (q * (HEAD_DIM ** -0.5)).reshape(bsz, nb, BLOCK, A_KV_HEADS, grp, HEAD_DIM)

    def band(t):
        t = t.reshape(bsz, s, A_KV_HEADS, HEAD_DIM)
        tp = jnp.pad(t, ((0, 0), (BLOCK, 0), (0, 0), (0, 0))).reshape(bsz, nb + 1, BLOCK, A_KV_HEADS, HEAD_DIM)
        return jnp.concatenate([tp[:, :-1], tp[:, 1:]], axis=2)

    kb, vb = band(k), band(v)
    scores = jnp.einsum('bnqhgd,bnkhd->bnhgqk', q, kb).astype(jnp.float32)

    qi = jnp.arange(BLOCK)[:, None]
    ki = jnp.arange(2 * BLOCK)[None, :]
    dist = qi + BLOCK - ki
    in_win = (dist >= 0) & (dist < WINDOW)
    key_pos = jnp.arange(nb)[:, None, None] * BLOCK + ki[None] - BLOCK
    mask = in_win[None] & (key_pos >= 0)

    bias = rel_bias[t5_bucket(jnp.clip(dist, 0, None))]
    bias = bias.transpose(2, 0, 1).reshape(A_KV_HEADS, grp, BLOCK, 2 * BLOCK).astype(jnp.float32)
    scores = jnp.where(mask[None, :, None, None], scores + bias, -jnp.inf)

    sink = sinks.reshape(A_KV_HEADS, grp)[..., None, None].astype(jnp.float32)
    m = jnp.maximum(jnp.max(scores, axis=-1, keepdims=True), sink)
    p = jnp.exp(scores - m)
    p = p / (jnp.sum(p, axis=-1, keepdims=True) + jnp.exp(sink - m))
    o = jnp.einsum('bnhgqk,bnkhd->bnqhgd', p.astype(x.dtype), vb).reshape(bsz, s, A_HEADS * HEAD_DIM)
    return o @ w_o + b_o


def stick_breaking_attention(x, w_qkv, w_o):
    bsz, s, _ = x.shape
    nb = s // BLOCK
    q, k, v = jnp.split(x @ w_qkv, 3, axis=-1)
    q = (q * (HEAD_DIM ** -0.5)).reshape(bsz, nb, BLOCK, B_HEADS, HEAD_DIM).transpose(1, 0, 3, 2, 4)
    k = k.reshape(bsz, s, B_HEADS, HEAD_DIM).transpose(0, 2, 1, 3)
    v = v.reshape(bsz, s, B_HEADS, HEAD_DIM).transpose(0, 2, 1, 3)
    key_pos = jnp.arange(s)

    def one_block(args):
        i, qb = args
        z = jnp.einsum('bhqd,bhkd->bhqk', qb, k).astype(jnp.float32)
        q_pos = i * BLOCK + jnp.arange(BLOCK)
        causal = key_pos[None, :] < q_pos[:, None]
        log_1m_beta = jnp.where(causal, -jax.nn.softplus(z), 0.0)
        log_pass = lax.cumsum(log_1m_beta, axis=3, reverse=True) - log_1m_beta
        w = jnp.where(causal, jnp.exp(jax.nn.log_sigmoid(z) + log_pass), 0.0)
        return jnp.einsum('bhqk,bhkd->bhqd', w.astype(v.dtype), v)

    o = lax.map(one_block, (jnp.arange(nb), q))
    o = o.transpose(1, 0, 3, 2, 4).reshape(bsz, s, B_HEADS * HEAD_DIM)
    return o @ w_o


def moe(x, w_router, b_router, w_gate_up, b_gate_up, w_down, b_down):
    bsz, s, d = x.shape
    xt = x.reshape(-1, d)
    n_tok = xt.shape[0]
    nk = n_tok * TOP_K
    logits = (xt @ w_router + b_router).astype(jnp.float32)
    top_logits, top_idx = lax.top_k(logits, TOP_K)
    gates = jax.nn.softmax(top_logits, axis=-1)

    flat_e = top_idx.reshape(-1)
    flat_tok = jnp.arange(nk) // TOP_K
    flat_gate = gates.reshape(-1)
    order = jnp.argsort(flat_e)
    se = flat_e[order]
    counts = jnp.bincount(flat_e, length=N_EXPERTS)
    padded = (counts + MOE_ROWS - 1) // MOE_ROWS * MOE_ROWS
    pad_end = jnp.cumsum(padded)
    pad_start = pad_end - padded
    start = jnp.cumsum(counts) - counts
    dest = pad_start[se] + jnp.arange(nk) - start[se]

    n_blocks = -(-nk // MOE_ROWS) + N_EXPERTS
    n_slots = n_blocks * MOE_ROWS
    slot_tok = jnp.zeros((n_slots,), jnp.int32).at[dest].set(flat_tok[order])
    slot_gate = jnp.zeros((n_slots,), jnp.float32).at[dest].set(flat_gate[order])
    block_e = jnp.minimum(jnp.searchsorted(pad_end, jnp.arange(n_blocks) * MOE_ROWS, side='right'),
                          N_EXPERTS - 1)

    def expert_block(args):
        tok, gate, e = args
        h = xt[tok] @ w_gate_up[e] + b_gate_up[e]
        g, u = jnp.split(h, 2, axis=-1)
        g = jnp.minimum(g, SWIGLU_LIMIT)
        u = jnp.clip(u, -SWIGLU_LIMIT, SWIGLU_LIMIT)
        act = (u + 1.0) * (g * jax.nn.sigmoid(SWIGLU_ALPHA * g))
        y = act @ w_down[e] + b_down[e]
        return y * gate[:, None].astype(y.dtype)

    ys = lax.map(expert_block, (slot_tok.reshape(n_blocks, MOE_ROWS),
                                slot_gate.reshape(n_blocks, MOE_ROWS), block_e))
    out = jnp.zeros_like(xt).at[slot_tok].add(ys.reshape(n_slots, d))
    return out.reshape(bsz, s, d)


def setup_inputs(seed: int = 0) -> dict:
    key = jax.random.key(seed)
    ks = jax.random.split(key, 24)
    f32 = jnp.float32
    qkv_a = (A_HEADS + 2 * A_KV_HEADS) * HEAD_DIM
    qkv_b = 3 * B_HEADS * HEAD_DIM
    a_col_scale = jnp.concatenate([jnp.ones(((A_HEADS + A_KV_HEADS) * HEAD_DIM,), f32),
                                   jnp.full((A_KV_HEADS * HEAD_DIM,), DN_BETA, f32)])
    b_col_scale = jnp.concatenate([jnp.ones((2 * B_HEADS * HEAD_DIM,), f32),
                                   jnp.full((B_HEADS * HEAD_DIM,), DN_BETA, f32)])
    nrm = lambda k, shape: jax.random.normal(k, shape, f32)
    return {
        'x': nrm(ks[0], (BATCH, SEQ, D_MODEL)),
        'rel_bias': 0.2 * nrm(ks[1], (N_BUCKETS, A_HEADS)),
        'a_w_qkv': nrm(ks[2], (N_A_LAYERS, D_MODEL, qkv_a)) * D_MODEL ** -0.5 * a_col_scale,
        'a_b_qkv': 0.02 * nrm(ks[3], (N_A_LAYERS, qkv_a)),
        'a_sinks': nrm(ks[4], (N_A_LAYERS, A_HEADS)),
        'a_w_o': nrm(ks[5], (N_A_LAYERS, A_HEADS * HEAD_DIM, D_MODEL)) * (A_HEADS * HEAD_DIM) ** -0.5 * DN_BETA,
        'a_b_o': 0.02 * nrm(ks[6], (N_A_LAYERS, D_MODEL)),
        'b_w_qkv': nrm(ks[7], (N_B_LAYERS, D_MODEL, qkv_b)) * D_MODEL ** -0.5 * b_col_scale,
        'b_w_o': nrm(ks[8], (N_B_LAYERS, B_HEADS * HEAD_DIM, D_MODEL)) * (B_HEADS * HEAD_DIM) ** -0.5 * DN_BETA,
        'ln1_g': 1.0 + 0.02 * nrm(ks[9], (DEPTH, D_MODEL)),
        'ln1_b': 0.02 * nrm(ks[10], (DEPTH, D_MODEL)),
        'ln2_g': 1.0 + 0.02 * nrm(ks[11], (DEPTH, D_MODEL)),
        'ln2_b': 0.02 * nrm(ks[12], (DEPTH, D_MODEL)),
        'moe_w_router': nrm(ks[13], (DEPTH, D_MODEL, N_EXPERTS)) * D_MODEL ** -0.5,
        'moe_b_router': 0.01 * nrm(ks[14], (DEPTH, N_EXPERTS)),
        'moe_w_gate_up': nrm(ks[15], (DEPTH, N_EXPERTS, D_MODEL, 2 * D_FF)) * D_MODEL ** -0.5,
        'moe_b_gate_up': 0.02 * nrm(ks[16], (DEPTH, N_EXPERTS, 2 * D_FF)),
        'moe_w_down': nrm(ks[17], (DEPTH, N_EXPERTS, D_FF, D_MODEL)) * D_FF ** -0.5 * DN_BETA,
        'moe_b_down': 0.02 * nrm(ks[18], (DEPTH, N_EXPERTS, D_MODEL)),
    }


def reference(x, rel_bias, a_w_qkv, a_b_qkv, a_sinks, a_w_o, a_b_o, b_w_qkv, b_w_o,
              ln1_g, ln1_b, ln2_g, ln2_b, moe_w_router, moe_b_router,
              moe_w_gate_up, moe_b_gate_up, moe_w_down, moe_b_down):
    for i in range(DEPTH):
        j = i // N_MIXERS
        if i % N_MIXERS == 0:
            y = sliding_window_attention(x, a_w_qkv[j], a_b_qkv[j], a_sinks[j], a_w_o[j], a_b_o[j], rel_bias)
        else:
            y = stick_breaking_attention(x, b_w_qkv[j], b_w_o[j])
        x = layer_norm(DN_ALPHA * x + y, ln1_g[i], ln1_b[i])
        y = moe(x, moe_w_router[i], moe_b_router[i], moe_w_gate_up[i], moe_b_gate_up[i],
                moe_w_down[i], moe_b_down[i])
        x = layer_norm(DN_ALPHA * x + y, ln2_g[i], ln2_b[i])
    return x
```

```python
import functools
import math

import jax
import jax.numpy as jnp
from jax import lax
from jax.experimental import pallas as pl
from jax.experimental.pallas import tpu as pltpu

D_MODEL = 1024
HEAD_DIM = 64
N_HEADS = 16
N_PAIRS = N_HEADS // 2
BLOCK = 128
A_KV_HEADS = 2
WINDOW = 128
N_BUCKETS = 32
MAX_DISTANCE = WINDOW
N_EXPERTS = 32
TOP_K = 4
D_FF = 1024
SWIGLU_LIMIT = 7.0
SWIGLU_ALPHA = 1.702
DEPTH = 4
DN_ALPHA = (2 * DEPTH) ** 0.25
LN_EPS = 1e-5

LANES = 128
MOE_ROWS = 256
TOK_TILE = 256
PROJ_ROWS = 512
PROJ_COLS = 512
FF_CHUNK = 256
ROUTER_PAD = 128
MASK_VALUE = -1e30
SB_LOG_FLOOR = -88.0
VMEM_LIMIT = 56 * 1024 * 1024

F32 = jnp.float32
BF16 = jnp.bfloat16


def _params(n_grid):
    return pltpu.CompilerParams(dimension_semantics=("arbitrary",) * n_grid,
                                vmem_limit_bytes=VMEM_LIMIT)


def _proj_kernel(x_ref, w_ref, b_ref, o_ref):
    x = x_ref[...].astype(BF16)
    n_out = o_ref.shape[1]
    for c in range(n_out // PROJ_COLS):
        cols = slice(c * PROJ_COLS, (c + 1) * PROJ_COLS)
        acc = jnp.dot(x, w_ref[:, cols], preferred_element_type=F32)
        o_ref[:, cols] = (acc + b_ref[:, cols]).astype(o_ref.dtype)


def _project(x, w, b):
    n, d = x.shape
    n_out = w.shape[1]
    return pl.pallas_call(
        _proj_kernel,
        out_shape=jax.ShapeDtypeStruct((n, n_out), BF16),
        grid=(n // PROJ_ROWS,),
        in_specs=[pl.BlockSpec((PROJ_ROWS, d), lambda i: (i, 0)),
                  pl.BlockSpec((d, n_out), lambda i: (0, 0)),
                  pl.BlockSpec((1, n_out), lambda i: (0, 0))],
        out_specs=pl.BlockSpec((PROJ_ROWS, n_out), lambda i: (i, 0)),
        compiler_params=_params(1),
        name="proj",
    )(x, w, b)


def _swa_kernel(nb, sinks_ref, q_ref, kp_ref, kc_ref, vp_ref, vc_ref, bias_ref, o_ref):
    first = (pl.program_id(0) % nb) == 0
    lane = lax.broadcasted_iota(jnp.int32, (BLOCK, LANES), 1)
    even = lane < HEAD_DIM
    col = lax.broadcasted_iota(jnp.int32, (BLOCK, 2 * BLOCK), 1)
    key_ok = jnp.logical_or(col >= BLOCK, jnp.logical_not(first))
    for p in range(N_PAIRS):
        g = p // (N_PAIRS // A_KV_HEADS)
        gl = slice(g * LANES, (g + 1) * LANES)
        qp = q_ref[:, p * LANES:(p + 1) * LANES]
        kd = jnp.concatenate([kp_ref[:, gl], kc_ref[:, gl]], axis=0)
        vd = jnp.concatenate([vp_ref[:, gl], vc_ref[:, gl]], axis=0)
        zero = jnp.zeros_like(qp)
        outs = []
        for hh, qm in enumerate((jnp.where(even, qp, zero), jnp.where(even, zero, qp))):
            h = 2 * p + hh
            s = lax.dot_general(qm, kd, (((1,), (1,)), ((), ())),
                                preferred_element_type=F32)
            s = jnp.where(key_ok, s + bias_ref[h], MASK_VALUE)
            sink = sinks_ref[h]
            m = jnp.maximum(jnp.max(s, axis=-1, keepdims=True), sink)
            e = jnp.exp(s - m)
            den = jnp.sum(e, axis=-1, keepdims=True) + jnp.exp(sink - m)
            pr = (e / den).astype(BF16)
            outs.append(jnp.dot(pr, vd, preferred_element_type=F32))
        o_ref[:, p * LANES:(p + 1) * LANES] = jnp.where(even, outs[0], outs[1]).astype(o_ref.dtype)


def _swa_attention(qkv, sinks, bias_tab, seq):
    n = qkv.shape[0]
    nb = seq // BLOCK
    q_cols = N_HEADS * HEAD_DIM
    kv_cols = 2 * LANES
    k_blk = q_cols // kv_cols
    v_blk = k_blk + 1

    def prev(i):
        return jnp.maximum(i - 1, 0)

    return pl.pallas_call(
        functools.partial(_swa_kernel, nb),
        out_shape=jax.ShapeDtypeStruct((n, q_cols), BF16),
        grid=(n // BLOCK,),
        in_specs=[pl.BlockSpec(memory_space=pltpu.SMEM),
                  pl.BlockSpec((BLOCK, q_cols), lambda i: (i, 0)),
                  pl.BlockSpec((BLOCK, kv_cols), lambda i: (prev(i), k_blk)),
                  pl.BlockSpec((BLOCK, kv_cols), lambda i: (i, k_blk)),
                  pl.BlockSpec((BLOCK, kv_cols), lambda i: (prev(i), v_blk)),
                  pl.BlockSpec((BLOCK, kv_cols), lambda i: (i, v_blk)),
                  pl.BlockSpec((N_HEADS, BLOCK, 2 * BLOCK), lambda i: (0, 0, 0))],
        out_specs=pl.BlockSpec((BLOCK, q_cols), lambda i: (i, 0)),
        compiler_params=_params(1),
        name="swa_attention",
    )(sinks, qkv, qkv, qkv, qkv, qkv, bias_tab)


def _t5_bucket(n):
    max_exact = N_BUCKETS // 2
    nf = jnp.maximum(n, 1).astype(F32)
    large = max_exact + (jnp.log(nf / max_exact) / math.log(MAX_DISTANCE / max_exact)
                         * (N_BUCKETS - max_exact)).astype(jnp.int32)
    large = jnp.minimum(large, N_BUCKETS - 1)
    return jnp.where(n < max_exact, n, large)


def _swa_bias_table(rel_bias):
    qi = jnp.arange(BLOCK)[:, None]
    ki = jnp.arange(2 * BLOCK)[None, :]
    dist = qi + BLOCK - ki
    in_win = (dist >= 0) & (dist < WINDOW)
    bias = rel_bias[_t5_bucket(jnp.clip(dist, 0, None))]
    bias = bias.transpose(2, 0, 1).astype(F32)
    return jnp.where(in_win[None], bias, MASK_VALUE)


def _sb_kernel(q_ref, k_ref, v_ref, tri_ref, o_ref, carry_ref, acc_ref):
    i = pl.program_id(2)
    lane = lax.broadcasted_iota(jnp.int32, (BLOCK, LANES), 1)
    even = lane < HEAD_DIM
    qp = q_ref[...]
    zero = jnp.zeros_like(qp)
    q2 = jnp.concatenate([jnp.where(even, qp, zero), jnp.where(even, zero, qp)], axis=0)
    row = lax.broadcasted_iota(jnp.int32, (2 * BLOCK, BLOCK), 0)
    kcol = lax.broadcasted_iota(jnp.int32, (2 * BLOCK, BLOCK), 1)
    causal = kcol < (row % BLOCK)
    tri = tri_ref[...]

    def key_block(j, diagonal):
        start = pl.multiple_of(j * BLOCK, BLOCK)
        kb = k_ref[pl.ds(start, BLOCK), :]
        vb = v_ref[pl.ds(start, BLOCK), :]
        z = lax.dot_general(q2, kb, (((1,), (1,)), ((), ())), preferred_element_type=F32)
        sp = jnp.maximum(z, 0.0) + jnp.log1p(jnp.exp(-jnp.abs(z)))
        log_1m_beta = -sp
        if diagonal:
            log_1m_beta = jnp.where(causal, log_1m_beta, 0.0)
        hi = log_1m_beta.astype(BF16)
        lo = (log_1m_beta - hi.astype(F32)).astype(BF16)
        cs = (jnp.dot(hi, tri, preferred_element_type=F32)
              + jnp.dot(lo, tri, preferred_element_type=F32))
        carry = carry_ref[...]
        log_w = (z - sp) + (carry + cs[:, :BLOCK])
        w = jnp.exp(log_w)
        if diagonal:
            w = jnp.where(causal, w, 0.0)
        acc_ref[...] += jnp.dot(w.astype(BF16), vb, preferred_element_type=F32)
        new_carry = carry + cs[:, BLOCK:]
        carry_ref[...] = new_carry
        return jnp.max(new_carry)

    carry_ref[...] = jnp.zeros_like(carry_ref)
    acc_ref[...] = jnp.zeros_like(acc_ref)
    top = key_block(i, True)

    def cond(state):
        j, top = state
        return jnp.logical_and(j >= 0, top > SB_LOG_FLOOR)

    def body(state):
        j, _ = state
        return j - 1, key_block(j, False)

    lax.while_loop(cond, body, (i - 1, top))
    acc = acc_ref[...]
    o_ref[...] = jnp.where(even, acc[:BLOCK], acc[BLOCK:]).astype(o_ref.dtype)


def _sb_attention(qkv, batch, seq):
    n = qkv.shape[0]
    nb = seq // BLOCK
    tri = jnp.concatenate(
        [(jnp.arange(BLOCK)[:, None] > jnp.arange(BLOCK)[None, :]).astype(BF16),
         jnp.ones((BLOCK, BLOCK), BF16)], axis=1)
    return pl.pallas_call(
        _sb_kernel,
        out_shape=jax.ShapeDtypeStruct((n, N_HEADS * HEAD_DIM), BF16),
        grid=(batch, N_PAIRS, nb),
        in_specs=[pl.BlockSpec((BLOCK, LANES), lambda b, p, i: (b * nb + i, p)),
                  pl.BlockSpec((seq, LANES), lambda b, p, i: (b, N_PAIRS + p)),
                  pl.BlockSpec((seq, LANES), lambda b, p, i: (b, 2 * N_PAIRS + p)),
                  pl.BlockSpec((BLOCK, 2 * BLOCK), lambda b, p, i: (0, 0))],
        out_specs=pl.BlockSpec((BLOCK, LANES), lambda b, p, i: (b * nb + i, p)),
        scratch_shapes=[pltpu.VMEM((2 * BLOCK, LANES), F32),
                        pltpu.VMEM((2 * BLOCK, LANES), F32)],
        compiler_params=_params(3),
        name="sb_attention",
    )(qkv, qkv, qkv, tri)


def _layer_norm(h, g, b):
    mu = jnp.mean(h, axis=-1, keepdims=True)
    xc = h - mu
    var = jnp.mean(xc * xc, axis=-1, keepdims=True)
    return xc * lax.rsqrt(var + LN_EPS) * g + b


def _post_attn_kernel(o_ref, x_ref, wo_ref, bo_ref, g_ref, b_ref, wr_ref, br_ref,
                      x1_ref, lg_ref):
    y = jnp.dot(o_ref[...], wo_ref[...], preferred_element_type=F32) + bo_ref[...]
    x1 = _layer_norm(DN_ALPHA * x_ref[...] + y, g_ref[...], b_ref[...])
    x1_ref[...] = x1
    lg_ref[...] = jnp.dot(x1, wr_ref[...], preferred_element_type=F32,
                          precision=lax.Precision.HIGHEST) + br_ref[...]


def _post_attention(o, x, wo, bo, g, b, wr, br):
    n, d = x.shape
    row = lambda i: (i, 0)
    fixed = lambda i: (0, 0)
    return pl.pallas_call(
        _post_attn_kernel,
        out_shape=(jax.ShapeDtypeStruct((n, d), F32),
                   jax.ShapeDtypeStruct((n, ROUTER_PAD), F32)),
        grid=(n // TOK_TILE,),
        in_specs=[pl.BlockSpec((TOK_TILE, d), row),
                  pl.BlockSpec((TOK_TILE, d), row),
                  pl.BlockSpec((d, d), fixed),
                  pl.BlockSpec((1, d), fixed),
                  pl.BlockSpec((1, d), fixed),
                  pl.BlockSpec((1, d), fixed),
                  pl.BlockSpec((d, ROUTER_PAD), fixed),
                  pl.BlockSpec((1, ROUTER_PAD), fixed)],
        out_specs=(pl.BlockSpec((TOK_TILE, d), row),
                   pl.BlockSpec((TOK_TILE, ROUTER_PAD), row)),
        compiler_params=_params(1),
        name="post_attention",
    )(o, x, wo, bo, g, b, wr, br)


def _dispatch_kernel(dest_ref, x_ref, xs_in_ref, xs_ref, sem):
    del xs_in_ref

    def issue(t, carry):
        for k in range(TOP_K):
            d = dest_ref[0, 0, t * TOP_K + k]
            pltpu.make_async_copy(x_ref.at[pl.ds(t, 1), :], xs_ref.at[pl.ds(d, 1), :], sem).start()
        return carry

    lax.fori_loop(0, TOK_TILE, issue, 0)
    for _ in range(TOP_K):
        pltpu.make_async_copy(x_ref, xs_ref.at[pl.ds(0, TOK_TILE), :], sem).wait()


def _dispatch(dest, x1, n_slots):
    n, d = x1.shape
    tiles = n // TOK_TILE
    return pl.pallas_call(
        _dispatch_kernel,
        out_shape=jax.ShapeDtypeStruct((n_slots, d), F32),
        grid=(tiles,),
        in_specs=[pl.BlockSpec((1, 1, TOK_TILE * TOP_K), lambda i: (i, 0, 0),
                               memory_space=pltpu.SMEM),
                  pl.BlockSpec((TOK_TILE, d), lambda i: (i, 0)),
                  pl.BlockSpec(memory_space=pl.ANY)],
        out_specs=pl.BlockSpec(memory_space=pl.ANY),
        scratch_shapes=[pltpu.SemaphoreType.DMA],
        input_output_aliases={2: 0},
        compiler_params=_params(1),
        name="moe_dispatch",
    )(dest.reshape(tiles, 1, TOK_TILE * TOP_K), x1, jnp.zeros((n_slots, d), F32))


def _expert_kernel(be_ref, nused_ref, xs_ref, wgu_ref, bgu_ref, wd_ref, bd_ref, ys_ref):
    del be_ref
    used = pl.program_id(0) < nused_ref[0]

    @pl.when(used)
    def _():
        x = xs_ref[...].astype(BF16)
        y = jnp.zeros(ys_ref.shape, F32)
        for c in range(D_FF // FF_CHUNK):
            gc = slice(c * FF_CHUNK, (c + 1) * FF_CHUNK)
            uc = slice(D_FF + c * FF_CHUNK, D_FF + (c + 1) * FF_CHUNK)
            g = jnp.dot(x, wgu_ref[:, gc], preferred_element_type=F32) + bgu_ref[:, gc]
            u = jnp.dot(x, wgu_ref[:, uc], preferred_element_type=F32) + bgu_ref[:, uc]
            g = jnp.minimum(g, SWIGLU_LIMIT)
            u = jnp.clip(u, -SWIGLU_LIMIT, SWIGLU_LIMIT)
            act = (u + 1.0) * (g * jax.nn.sigmoid(SWIGLU_ALPHA * g))
            y = y + jnp.dot(act.astype(BF16), wd_ref[gc, :], preferred_element_type=F32)
        ys_ref[...] = y + bd_ref[...]

    @pl.when(jnp.logical_not(used))
    def _():
        ys_ref[...] = jnp.zeros_like(ys_ref)


def _experts(block_e, n_used, xs, wgu, bgu, wd, bd):
    n_slots, d = xs.shape
    n_blocks = n_slots // MOE_ROWS
    grid_spec = pltpu.PrefetchScalarGridSpec(
        num_scalar_prefetch=2,
        grid=(n_blocks,),
        in_specs=[pl.BlockSpec((MOE_ROWS, d), lambda i, be, nu: (i, 0)),
                  pl.BlockSpec((None, d, 2 * D_FF), lambda i, be, nu: (be[i], 0, 0)),
                  pl.BlockSpec((None, 1, 2 * D_FF), lambda i, be, nu: (be[i], 0, 0)),
                  pl.BlockSpec((None, D_FF, d), lambda i, be, nu: (be[i], 0, 0)),
                  pl.BlockSpec((None, 1, d), lambda i, be, nu: (be[i], 0, 0))],
        out_specs=pl.BlockSpec((MOE_ROWS, d), lambda i, be, nu: (i, 0)),
    )
    return pl.pallas_call(
        _expert_kernel,
        out_shape=jax.ShapeDtypeStruct((n_slots, d), F32),
        grid_spec=grid_spec,
        compiler_params=_params(1),
        name="moe_experts",
    )(block_e, n_used, xs, wgu, bgu, wd, bd)


def _combine_kernel(dest_ref, gate_ref, x_ref, g_ref, b_ref, ys_ref, o_ref, buf, sem):
    def issue(t, carry):
        for k in range(TOP_K):
            d = dest_ref[0, 0, t * TOP_K + k]
            pltpu.make_async_copy(ys_ref.at[pl.ds(d, 1), :], buf.at[k, pl.ds(t, 1), :], sem).start()
        return carry

    lax.fori_loop(0, TOK_TILE, issue, 0)
    for k in range(TOP_K):
        pltpu.make_async_copy(ys_ref.at[pl.ds(0, TOK_TILE), :], buf.at[k], sem).wait()
    gates = gate_ref[...]
    y = buf[0] * gates[:, 0:1]
    for k in range(1, TOP_K):
        y = y + buf[k] * gates[:, k:k + 1]
    o_ref[...] = _layer_norm(DN_ALPHA * x_ref[...] + y, g_ref[...], b_ref[...])


def _combine(dest, gates, x1, g, b, ys):
    n, d = x1.shape
    tiles = n // TOK_TILE
    row = lambda i: (i, 0)
    fixed = lambda i: (0, 0)
    return pl.pallas_call(
        _combine_kernel,
        out_shape=jax.ShapeDtypeStruct((n, d), F32),
        grid=(tiles,),
        in_specs=[pl.BlockSpec((1, 1, TOK_TILE * TOP_K), lambda i: (i, 0, 0),
                               memory_space=pltpu.SMEM),
                  pl.BlockSpec((TOK_TILE, TOP_K), row),
                  pl.BlockSpec((TOK_TILE, d), row),
                  pl.BlockSpec((1, d), fixed),
                  pl.BlockSpec((1, d), fixed),
                  pl.BlockSpec(memory_space=pl.ANY)],
        out_specs=pl.BlockSpec((TOK_TILE, d), row),
        scratch_shapes=[pltpu.VMEM((TOP_K, TOK_TILE, d), F32), pltpu.SemaphoreType.DMA],
        compiler_params=_params(1),
        name="moe_combine",
    )(dest.reshape(tiles, 1, TOK_TILE * TOP_K), gates, x1, g, b, ys)


def _route(logits):
    n = logits.shape[0]
    top_logits, top_idx = lax.top_k(logits, TOP_K)
    gates = jax.nn.softmax(top_logits, axis=-1)
    chosen = (top_idx[:, :, None] == jnp.arange(N_EXPERTS)[None, None, :]).sum(axis=1).astype(jnp.int32)
    rank = jnp.cumsum(chosen, axis=0) - chosen
    counts = chosen.sum(axis=0)
    padded = (counts + MOE_ROWS - 1) // MOE_ROWS * MOE_ROWS
    pad_end = jnp.cumsum(padded)
    pad_start = pad_end - padded
    dest = pad_start[top_idx] + jnp.take_along_axis(rank, top_idx, axis=1)
    n_blocks = -(-(n * TOP_K) // MOE_ROWS) + N_EXPERTS
    block_e = jnp.minimum(
        jnp.searchsorted(pad_end, jnp.arange(n_blocks) * MOE_ROWS, side='right'), N_EXPERTS - 1)
    n_used = (pad_end[-1] // MOE_ROWS).reshape(1)
    return gates, dest.astype(jnp.int32), block_e.astype(jnp.int32), n_used.astype(jnp.int32), n_blocks


def _moe_layer(x1, logits, wgu, bgu, wd, bd, g, b):
    gates, dest, block_e, n_used, n_blocks = _route(logits)
    xs = _dispatch(dest, x1, n_blocks * MOE_ROWS)
    ys = _experts(block_e, n_used, xs, wgu, bgu, wd, bd)
    return _combine(dest, gates, x1, g, b, ys)


def kernel(x, rel_bias, a_w_qkv, a_b_qkv, a_sinks, a_w_o, a_b_o, b_w_qkv, b_w_o, ln1_g, ln1_b, ln2_g, ln2_b, moe_w_router, moe_b_router, moe_w_gate_up, moe_b_gate_up, moe_w_down, moe_b_down):
    batch, seq, d = x.shape
    n = batch * seq
    q_cols = N_HEADS * HEAD_DIM
    q_scale = HEAD_DIM ** -0.5
    xt = x.reshape(n, d)
    bias_tab = _swa_bias_table(rel_bias)
    row = lambda v: v.reshape(1, -1).astype(F32)

    def dup_heads(t):
        parts = [t[..., h * HEAD_DIM:(h + 1) * HEAD_DIM] for h in range(A_KV_HEADS)]
        return jnp.concatenate([parts[0], parts[0], parts[1], parts[1]], axis=-1)

    for i in range(DEPTH):
        j = i // 2
        if i % 2 == 0:
            w, bq = a_w_qkv[j], a_b_qkv[j]
            kv = A_KV_HEADS * HEAD_DIM
            w_all = jnp.concatenate([w[:, :q_cols] * q_scale,
                                     dup_heads(w[:, q_cols:q_cols + kv]),
                                     dup_heads(w[:, q_cols + kv:])], axis=1).astype(BF16)
            b_all = jnp.concatenate([bq[:q_cols] * q_scale,
                                     dup_heads(bq[q_cols:q_cols + kv]),
                                     dup_heads(bq[q_cols + kv:])]).reshape(1, -1)
            qkv = _project(xt, w_all, b_all)
            o = _swa_attention(qkv, a_sinks[j].astype(F32), bias_tab, seq)
            wo, bo = a_w_o[j].astype(BF16), row(a_b_o[j])
        else:
            w = b_w_qkv[j]
            w_all = jnp.concatenate([w[:, :q_cols] * q_scale, w[:, q_cols:]], axis=1).astype(BF16)
            qkv = _project(xt, w_all, jnp.zeros((1, w_all.shape[1]), F32))
            o = _sb_attention(qkv, batch, seq)
            wo, bo = b_w_o[j].astype(BF16), jnp.zeros((1, d), F32)
        wr = jnp.pad(moe_w_router[i], ((0, 0), (0, ROUTER_PAD - N_EXPERTS)))
        br = jnp.pad(moe_b_router[i], (0, ROUTER_PAD - N_EXPERTS)).reshape(1, -1)
        x1, logits = _post_attention(o, xt, wo, bo, row(ln1_g[i]), row(ln1_b[i]), wr, br)
        xt = _moe_layer(x1, logits[:, :N_EXPERTS],
                        moe_w_gate_up[i].astype(BF16), moe_b_gate_up[i].reshape(N_EXPERTS, 1, -1),
                        moe_w_down[i].astype(BF16), moe_b_down[i].reshape(N_EXPERTS, 1, -1),
                        row(ln2_g[i]), row(ln2_b[i]))
    return xt.reshape(batch, seq, d)
```

```python
import functools
import math

import jax
import jax.numpy as jnp
from jax import lax
from jax.experimental import pallas as pl
from jax.experimental.pallas import tpu as pltpu

D_MODEL = 1024
HEAD_DIM = 64
N_HEADS = 16
N_PAIRS = N_HEADS // 2
BLOCK = 128
A_KV_HEADS = 2
WINDOW = 128
N_BUCKETS = 32
MAX_DISTANCE = WINDOW
N_EXPERTS = 32
TOP_K = 4
D_FF = 1024
SWIGLU_LIMIT = 7.0
SWIGLU_ALPHA = 1.702
DEPTH = 4
DN_ALPHA = (2 * DEPTH) ** 0.25
LN_EPS = 1e-5

LANES = 128
MOE_ROWS = 512
ROW_TILE = 8
DMA_UNROLL = 8
TOK_TILE = 256
PROJ_ROWS = 512
PROJ_COLS = 512
FF_CHUNK = 256
ROUTER_PAD = 128
MASK_VALUE = -1e30
SB_LOG_FLOOR = -88.0
SB_PAIRS = 2
VMEM_LIMIT = 56 * 1024 * 1024

F32 = jnp.float32
BF16 = jnp.bfloat16


def _params(n_grid):
    return pltpu.CompilerParams(dimension_semantics=("arbitrary",) * n_grid,
                                vmem_limit_bytes=VMEM_LIMIT)


def _proj_kernel(x_ref, w_ref, b_ref, o_ref):
    x = x_ref[...].astype(BF16)
    n_out = o_ref.shape[1]
    for c in range(n_out // PROJ_COLS):
        cols = slice(c * PROJ_COLS, (c + 1) * PROJ_COLS)
        acc = jnp.dot(x, w_ref[:, cols], preferred_element_type=F32)
        o_ref[:, cols] = (acc + b_ref[:, cols]).astype(o_ref.dtype)


def _project(x, w, b):
    n, d = x.shape
    n_out = w.shape[1]
    return pl.pallas_call(
        _proj_kernel,
        out_shape=jax.ShapeDtypeStruct((n, n_out), BF16),
        grid=(n // PROJ_ROWS,),
        in_specs=[pl.BlockSpec((PROJ_ROWS, d), lambda i: (i, 0)),
                  pl.BlockSpec((d, n_out), lambda i: (0, 0)),
                  pl.BlockSpec((1, n_out), lambda i: (0, 0))],
        out_specs=pl.BlockSpec((PROJ_ROWS, n_out), lambda i: (i, 0)),
        compiler_params=_params(1),
        name="proj",
    )(x, w, b)


def _swa_kernel(nb, sinks_ref, q_ref, kp_ref, kc_ref, vp_ref, vc_ref, bias_ref, o_ref):
    first = (pl.program_id(0) % nb) == 0
    lane = lax.broadcasted_iota(jnp.int32, (BLOCK, LANES), 1)
    even = lane < HEAD_DIM
    col = lax.broadcasted_iota(jnp.int32, (BLOCK, 2 * BLOCK), 1)
    key_ok = jnp.logical_or(col >= BLOCK, jnp.logical_not(first))
    for p in range(N_PAIRS):
        g = p // (N_PAIRS // A_KV_HEADS)
        gl = slice(g * LANES, (g + 1) * LANES)
        qp = q_ref[:, p * LANES:(p + 1) * LANES]
        kd = jnp.concatenate([kp_ref[:, gl], kc_ref[:, gl]], axis=0)
        vd = jnp.concatenate([vp_ref[:, gl], vc_ref[:, gl]], axis=0)
        zero = jnp.zeros_like(qp)
        outs = []
        for hh, qm in enumerate((jnp.where(even, qp, zero), jnp.where(even, zero, qp))):
            h = 2 * p + hh
            s = lax.dot_general(qm, kd, (((1,), (1,)), ((), ())),
                                preferred_element_type=F32)
            s = jnp.where(key_ok, s + bias_ref[h], MASK_VALUE)
            sink = sinks_ref[h]
            m = jnp.maximum(jnp.max(s, axis=-1, keepdims=True), sink)
            e = jnp.exp(s - m)
            den = jnp.sum(e, axis=-1, keepdims=True) + jnp.exp(sink - m)
            pr = (e / den).astype(BF16)
            outs.append(jnp.dot(pr, vd, preferred_element_type=F32))
        o_ref[:, p * LANES:(p + 1) * LANES] = jnp.where(even, outs[0], outs[1]).astype(o_ref.dtype)


def _swa_attention(qkv, sinks, bias_tab, seq):
    n = qkv.shape[0]
    nb = seq // BLOCK
    q_cols = N_HEADS * HEAD_DIM
    kv_cols = 2 * LANES
    k_blk = q_cols // kv_cols
    v_blk = k_blk + 1

    def prev(i):
        return jnp.maximum(i - 1, 0)

    return pl.pallas_call(
        functools.partial(_swa_kernel, nb),
        out_shape=jax.ShapeDtypeStruct((n, q_cols), BF16),
        grid=(n // BLOCK,),
        in_specs=[pl.BlockSpec(memory_space=pltpu.SMEM),
                  pl.BlockSpec((BLOCK, q_cols), lambda i: (i, 0)),
                  pl.BlockSpec((BLOCK, kv_cols), lambda i: (prev(i), k_blk)),
                  pl.BlockSpec((BLOCK, kv_cols), lambda i: (i, k_blk)),
                  pl.BlockSpec((BLOCK, kv_cols), lambda i: (prev(i), v_blk)),
                  pl.BlockSpec((BLOCK, kv_cols), lambda i: (i, v_blk)),
                  pl.BlockSpec((N_HEADS, BLOCK, 2 * BLOCK), lambda i: (0, 0, 0))],
        out_specs=pl.BlockSpec((BLOCK, q_cols), lambda i: (i, 0)),
        compiler_params=_params(1),
        name="swa_attention",
    )(sinks, qkv, qkv, qkv, qkv, qkv, bias_tab)


def _t5_bucket(n):
    max_exact = N_BUCKETS // 2
    nf = jnp.maximum(n, 1).astype(F32)
    large = max_exact + (jnp.log(nf / max_exact) / math.log(MAX_DISTANCE / max_exact)
                         * (N_BUCKETS - max_exact)).astype(jnp.int32)
    large = jnp.minimum(large, N_BUCKETS - 1)
    return jnp.where(n < max_exact, n, large)


def _swa_bias_table(rel_bias):
    qi = jnp.arange(BLOCK)[:, None]
    ki = jnp.arange(2 * BLOCK)[None, :]
    dist = qi + BLOCK - ki
    in_win = (dist >= 0) & (dist < WINDOW)
    bias = rel_bias[_t5_bucket(jnp.clip(dist, 0, None))]
    bias = bias.transpose(2, 0, 1).astype(F32)
    return jnp.where(in_win[None], bias, MASK_VALUE)


def _sb_kernel(q_ref, k_ref, v_ref, tri_ref, o_ref, carry_ref, acc_ref):
    i = pl.program_id(2)
    lane = lax.broadcasted_iota(jnp.int32, (BLOCK, LANES), 1)
    even = lane < HEAD_DIM
    tri_wide = tri_ref[...]
    tri_narrow = jnp.concatenate([tri_wide[:BLOCK, :BLOCK], tri_wide[:BLOCK, 2 * BLOCK:]], axis=1)
    start0 = pl.multiple_of(jnp.maximum(i - 1, 0) * BLOCK, BLOCK)
    row = lax.broadcasted_iota(jnp.int32, (2 * BLOCK, 2 * BLOCK), 0) % BLOCK
    col = lax.broadcasted_iota(jnp.int32, (2 * BLOCK, 2 * BLOCK), 1)
    causal = (start0 + col) < (i * BLOCK + row)

    def sweep(p, start, n_keys, first):
        cols = slice(p * LANES, (p + 1) * LANES)
        qp = q_ref[:, cols]
        zero = jnp.zeros_like(qp)
        q2 = jnp.concatenate([jnp.where(even, qp, zero), jnp.where(even, zero, qp)], axis=0)
        kb = k_ref[pl.ds(start, n_keys), cols]
        vb = v_ref[pl.ds(start, n_keys), cols]
        z = lax.dot_general(q2, kb, (((1,), (1,)), ((), ())), preferred_element_type=F32)
        sp = jnp.maximum(z, 0.0) + jnp.log1p(jnp.exp(-jnp.abs(z)))
        log_1m_beta = -sp
        if first:
            log_1m_beta = jnp.where(causal, log_1m_beta, 0.0)
        hi = log_1m_beta.astype(BF16)
        lo = (log_1m_beta - hi.astype(F32)).astype(BF16)
        tri = tri_wide if first else tri_narrow
        cs = (jnp.dot(hi, tri, preferred_element_type=F32)
              + jnp.dot(lo, tri, preferred_element_type=F32))
        log_w = (z - sp) + cs[:, :n_keys]
        if not first:
            log_w = log_w + carry_ref[p]
        w = jnp.exp(log_w)
        if first:
            w = jnp.where(causal, w, 0.0)
        pv = jnp.dot(w.astype(BF16), vb, preferred_element_type=F32)
        if first:
            acc_ref[p] = pv
            new_carry = cs[:, n_keys:]
        else:
            acc_ref[p] += pv
            new_carry = carry_ref[p] + cs[:, n_keys:]
        carry_ref[p] = new_carry
        return jnp.max(new_carry)

    def sweep_all(start, n_keys, first):
        tops = [sweep(p, start, n_keys, first) for p in range(SB_PAIRS)]
        return functools.reduce(jnp.maximum, tops)

    top = sweep_all(start0, 2 * BLOCK, True)

    def cond(state):
        j, top = state
        return jnp.logical_and(j >= 0, top > SB_LOG_FLOOR)

    def body(state):
        j, _ = state
        return j - 1, sweep_all(pl.multiple_of(j * BLOCK, BLOCK), BLOCK, False)

    lax.while_loop(cond, body, (i - 2, top))
    for p in range(SB_PAIRS):
        acc = acc_ref[p]
        o_ref[:, p * LANES:(p + 1) * LANES] = jnp.where(even, acc[:BLOCK], acc[BLOCK:]).astype(o_ref.dtype)


def _sb_attention(qkv, batch, seq):
    n = qkv.shape[0]
    nb = seq // BLOCK
    groups = N_PAIRS // SB_PAIRS
    width = SB_PAIRS * LANES
    tri = jnp.concatenate(
        [(jnp.arange(2 * BLOCK)[:, None] > jnp.arange(2 * BLOCK)[None, :]).astype(BF16),
         jnp.ones((2 * BLOCK, BLOCK), BF16)], axis=1)
    return pl.pallas_call(
        _sb_kernel,
        out_shape=jax.ShapeDtypeStruct((n, N_HEADS * HEAD_DIM), BF16),
        grid=(batch, groups, nb),
        in_specs=[pl.BlockSpec((BLOCK, width), lambda b, p, i: (b * nb + i, p)),
                  pl.BlockSpec((seq, width), lambda b, p, i: (b, groups + p)),
                  pl.BlockSpec((seq, width), lambda b, p, i: (b, 2 * groups + p)),
                  pl.BlockSpec((2 * BLOCK, 3 * BLOCK), lambda b, p, i: (0, 0))],
        out_specs=pl.BlockSpec((BLOCK, width), lambda b, p, i: (b * nb + i, p)),
        scratch_shapes=[pltpu.VMEM((SB_PAIRS, 2 * BLOCK, LANES), F32),
                        pltpu.VMEM((SB_PAIRS, 2 * BLOCK, LANES), F32)],
        compiler_params=_params(3),
        name="sb_attention",
    )(qkv, qkv, qkv, tri)


def _layer_norm(h, g, b):
    mu = jnp.mean(h, axis=-1, keepdims=True)
    xc = h - mu
    var = jnp.mean(xc * xc, axis=-1, keepdims=True)
    return xc * lax.rsqrt(var + LN_EPS) * g + b


def _to_row_tiles(ref, value):
    rows = value.shape[0]
    for s in range(ROW_TILE):
        ref[pl.ds(s, rows, stride=ROW_TILE), :] = value[:, s * LANES:(s + 1) * LANES]


def _from_row_tiles(ref, first, rows, stride):
    return jnp.concatenate(
        [ref[pl.ds(first + s, rows, stride=stride), :] for s in range(ROW_TILE)], axis=1)


def _post_attn_kernel(o_ref, x_ref, wo_ref, bo_ref, g_ref, b_ref, wr_ref, br_ref,
                      x1_ref, lg_ref):
    y = jnp.dot(o_ref[...], wo_ref[...], preferred_element_type=F32) + bo_ref[...]
    x1 = _layer_norm(DN_ALPHA * x_ref[...] + y, g_ref[...], b_ref[...])
    _to_row_tiles(x1_ref, x1)
    lg_ref[...] = jnp.dot(x1, wr_ref[...], preferred_element_type=F32,
                          precision=lax.Precision.HIGHEST) + br_ref[...]


def _post_attention(o, x, wo, bo, g, b, wr, br):
    n, d = x.shape
    row = lambda i: (i, 0)
    fixed = lambda i: (0, 0)
    return pl.pallas_call(
        _post_attn_kernel,
        out_shape=(jax.ShapeDtypeStruct((n * ROW_TILE, LANES), F32),
                   jax.ShapeDtypeStruct((n, ROUTER_PAD), F32)),
        grid=(n // TOK_TILE,),
        in_specs=[pl.BlockSpec((TOK_TILE, d), row),
                  pl.BlockSpec((TOK_TILE, d), row),
                  pl.BlockSpec((d, d), fixed),
                  pl.BlockSpec((1, d), fixed),
                  pl.BlockSpec((1, d), fixed),
                  pl.BlockSpec((1, d), fixed),
                  pl.BlockSpec((d, ROUTER_PAD), fixed),
                  pl.BlockSpec((1, ROUTER_PAD), fixed)],
        out_specs=(pl.BlockSpec((TOK_TILE * ROW_TILE, LANES), row),
                   pl.BlockSpec((TOK_TILE, ROUTER_PAD), row)),
        compiler_params=_params(1),
        name="post_attention",
    )(o, x, wo, bo, g, b, wr, br)


def _expert_kernel(be_ref, nused_ref, tok_ref, tok_next_ref, dst_ref, x1_ref,
                   wgu_ref, bgu_ref, wd_ref, bd_ref, y4_ref, xbuf, ybuf, gsem, ssem):
    del be_ref
    i = pl.program_id(0)
    n_steps = pl.num_programs(0)
    cur = i % 2
    nxt = 1 - cur
    block_tiles = MOE_ROWS * ROW_TILE

    def row_copy(src, src_row, dst, dst_row, sem):
        return pltpu.make_async_copy(
            src.at[pl.ds(pl.multiple_of(src_row * ROW_TILE, ROW_TILE), ROW_TILE), :],
            dst.at[pl.ds(pl.multiple_of(dst_row * ROW_TILE, ROW_TILE), ROW_TILE), :], sem)

    def start_gather(idx_ref, buf_slot):
        def body(c, carry):
            for u in range(DMA_UNROLL):
                r = c * DMA_UNROLL + u
                row_copy(x1_ref, idx_ref[0, 0, r], xbuf.at[buf_slot], r, gsem.at[buf_slot]).start()
            return carry
        lax.fori_loop(0, MOE_ROWS // DMA_UNROLL, body, 0)

    def start_scatter(buf_slot):
        def body(c, carry):
            for u in range(DMA_UNROLL):
                r = c * DMA_UNROLL + u
                row_copy(ybuf.at[buf_slot], r, y4_ref, dst_ref[0, 0, r], ssem.at[buf_slot]).start()
            return carry
        lax.fori_loop(0, MOE_ROWS // DMA_UNROLL, body, 0)

    def wait_gather(buf_slot):
        pltpu.make_async_copy(x1_ref.at[pl.ds(0, block_tiles), :], xbuf.at[buf_slot],
                              gsem.at[buf_slot]).wait()

    def wait_scatter(buf_slot):
        pltpu.make_async_copy(ybuf.at[buf_slot], y4_ref.at[pl.ds(0, block_tiles), :],
                              ssem.at[buf_slot]).wait()

    @pl.when(i == 0)
    def _():
        ybuf[...] = jnp.zeros_like(ybuf)
        start_gather(tok_ref, 0)

    @pl.when(i + 1 < n_steps)
    def _():
        start_gather(tok_next_ref, nxt)

    wait_gather(cur)

    @pl.when(i >= 2)
    def _():
        wait_scatter(cur)

    @pl.when(i < nused_ref[0])
    def _():
        x = _from_row_tiles(xbuf.at[cur], 0, MOE_ROWS, ROW_TILE).astype(BF16)
        y = jnp.zeros((MOE_ROWS, D_MODEL), F32)
        for c in range(D_FF // FF_CHUNK):
            gc = slice(c * FF_CHUNK, (c + 1) * FF_CHUNK)
            uc = slice(D_FF + c * FF_CHUNK, D_FF + (c + 1) * FF_CHUNK)
            g = jnp.dot(x, wgu_ref[:, gc], preferred_element_type=F32) + bgu_ref[:, gc]
            u = jnp.dot(x, wgu_ref[:, uc], preferred_element_type=F32) + bgu_ref[:, uc]
            g = jnp.minimum(g, SWIGLU_LIMIT)
            u = jnp.clip(u, -SWIGLU_LIMIT, SWIGLU_LIMIT)
            act = (u + 1.0) * (g * jax.nn.sigmoid(SWIGLU_ALPHA * g))
            y = y + jnp.dot(act.astype(BF16), wd_ref[gc, :], preferred_element_type=F32)
        _to_row_tiles(ybuf.at[cur], y + bd_ref[...])

    start_scatter(cur)

    @pl.when(i == n_steps - 1)
    def _():
        wait_scatter(nxt)
        wait_scatter(cur)


def _experts(block_e, n_used, slot_tok, slot_dst, x1, wgu, bgu, wd, bd, n_out_rows):
    n_blocks = slot_tok.shape[0]
    assert n_blocks >= 2
    d = D_MODEL
    last = n_blocks - 1
    idx_block = (1, 1, MOE_ROWS)
    grid_spec = pltpu.PrefetchScalarGridSpec(
        num_scalar_prefetch=2,
        grid=(n_blocks,),
        in_specs=[pl.BlockSpec(idx_block, lambda i, be, nu: (i, 0, 0), memory_space=pltpu.SMEM),
                  pl.BlockSpec(idx_block, lambda i, be, nu: (jnp.minimum(i + 1, last), 0, 0),
                               memory_space=pltpu.SMEM),
                  pl.BlockSpec(idx_block, lambda i, be, nu: (i, 0, 0), memory_space=pltpu.SMEM),
                  pl.BlockSpec(memory_space=pl.ANY),
                  pl.BlockSpec((None, d, 2 * D_FF), lambda i, be, nu: (be[i], 0, 0)),
                  pl.BlockSpec((None, 1, 2 * D_FF), lambda i, be, nu: (be[i], 0, 0)),
                  pl.BlockSpec((None, D_FF, d), lambda i, be, nu: (be[i], 0, 0)),
                  pl.BlockSpec((None, 1, d), lambda i, be, nu: (be[i], 0, 0))],
        out_specs=pl.BlockSpec(memory_space=pl.ANY),
        scratch_shapes=[pltpu.VMEM((2, MOE_ROWS * ROW_TILE, LANES), F32),
                        pltpu.VMEM((2, MOE_ROWS * ROW_TILE, LANES), F32),
                        pltpu.SemaphoreType.DMA((2,)),
                        pltpu.SemaphoreType.DMA((2,))],
    )
    return pl.pallas_call(
        _expert_kernel,
        out_shape=jax.ShapeDtypeStruct((n_out_rows * ROW_TILE, LANES), F32),
        grid_spec=grid_spec,
        compiler_params=_params(1),
        name="moe_experts",
    )(block_e, n_used, slot_tok, slot_tok, slot_dst, x1, wgu, bgu, wd, bd)


def _combine_kernel(gate_ref, x_ref, g_ref, b_ref, y4_ref, o_ref):
    gates = gate_ref[...]
    y = None
    for k in range(TOP_K):
        yk = _from_row_tiles(y4_ref, k * ROW_TILE, TOK_TILE, TOP_K * ROW_TILE) * gates[:, k:k + 1]
        y = yk if y is None else y + yk
    x1 = _from_row_tiles(x_ref, 0, TOK_TILE, ROW_TILE)
    o_ref[...] = _layer_norm(DN_ALPHA * x1 + y, g_ref[...], b_ref[...])


def _combine(gates, x1, g, b, y4):
    n = gates.shape[0]
    d = D_MODEL
    row = lambda i: (i, 0)
    fixed = lambda i: (0, 0)
    return pl.pallas_call(
        _combine_kernel,
        out_shape=jax.ShapeDtypeStruct((n, d), F32),
        grid=(n // TOK_TILE,),
        in_specs=[pl.BlockSpec((TOK_TILE, TOP_K), row),
                  pl.BlockSpec((TOK_TILE * ROW_TILE, LANES), row),
                  pl.BlockSpec((1, d), fixed),
                  pl.BlockSpec((1, d), fixed),
                  pl.BlockSpec((TOK_TILE * TOP_K * ROW_TILE, LANES), row)],
        out_specs=pl.BlockSpec((TOK_TILE, d), row),
        compiler_params=_params(1),
        name="moe_combine",
    )(gates, x1, g, b, y4)


def _route(logits):
    n = logits.shape[0]
    nk = n * TOP_K
    top_logits, top_idx = lax.top_k(logits, TOP_K)
    gates = jax.nn.softmax(top_logits, axis=-1)
    flat_e = top_idx.reshape(-1).astype(jnp.int32)
    order = jnp.sort(flat_e * nk + jnp.arange(nk, dtype=jnp.int32)) % nk
    experts = jnp.arange(N_EXPERTS, dtype=jnp.int32)
    counts = (flat_e[:, None] == experts[None, :]).sum(axis=0).astype(jnp.int32)
    padded = (counts + MOE_ROWS - 1) // MOE_ROWS * MOE_ROWS
    pad_end = jnp.cumsum(padded)
    pad_start = pad_end - padded
    start = jnp.cumsum(counts) - counts
    n_blocks = nk // MOE_ROWS + N_EXPERTS
    block_e = jnp.minimum(
        jnp.searchsorted(pad_end, jnp.arange(n_blocks) * MOE_ROWS, side='right'),
        N_EXPERTS - 1).astype(jnp.int32)
    n_used = (pad_end[-1] // MOE_ROWS).reshape(1).astype(jnp.int32)
    blk = jnp.arange(n_blocks, dtype=jnp.int32)[:, None]
    r = jnp.arange(MOE_ROWS, dtype=jnp.int32)[None, :]
    e = block_e[:, None]
    idx = blk * MOE_ROWS + r - pad_start[e]
    valid = idx < counts[e]
    src = order[jnp.clip(start[e] + idx, 0, nk - 1)]
    slot_tok = jnp.where(valid, src // TOP_K, 0).astype(jnp.int32)
    slot_dst = jnp.where(valid, src, nk + (blk % 2) * MOE_ROWS + r).astype(jnp.int32)
    return gates, slot_tok[:, None, :], slot_dst[:, None, :], block_e, n_used


def _moe_layer(x1, logits, wgu, bgu, wd, bd, g, b):
    n = logits.shape[0]
    gates, slot_tok, slot_dst, block_e, n_used = _route(logits)
    y4 = _experts(block_e, n_used, slot_tok, slot_dst, x1, wgu, bgu, wd, bd,
                  n * TOP_K + 2 * MOE_ROWS)
    return _combine(gates, x1, g, b, y4)


def kernel(x, rel_bias, a_w_qkv, a_b_qkv, a_sinks, a_w_o, a_b_o, b_w_qkv, b_w_o, ln1_g, ln1_b, ln2_g, ln2_b, moe_w_router, moe_b_router, moe_w_gate_up, moe_b_gate_up, moe_w_down, moe_b_down):
    batch, seq, d = x.shape
    n = batch * seq
    q_cols = N_HEADS * HEAD_DIM
    q_scale = HEAD_DIM ** -0.5
    xt = x.reshape(n, d)
    bias_tab = _swa_bias_table(rel_bias)
    row = lambda v: v.reshape(1, -1).astype(F32)

    def dup_heads(t):
        parts = [t[..., h * HEAD_DIM:(h + 1) * HEAD_DIM] for h in range(A_KV_HEADS)]
        return jnp.concatenate([parts[0], parts[0], parts[1], parts[1]], axis=-1)

    for i in range(DEPTH):
        j = i // 2
        if i % 2 == 0:
            w, bq = a_w_qkv[j], a_b_qkv[j]
            kv = A_KV_HEADS * HEAD_DIM
            w_all = jnp.concatenate([w[:, :q_cols] * q_scale,
                                     dup_heads(w[:, q_cols:q_cols + kv]),
                                     dup_heads(w[:, q_cols + kv:])], axis=1).astype(BF16)
            b_all = jnp.concatenate([bq[:q_cols] * q_scale,
                                     dup_heads(bq[q_cols:q_cols + kv]),
                                     dup_heads(bq[q_cols + kv:])]).reshape(1, -1)
            qkv = _project(xt, w_all, b_all)
            o = _swa_attention(qkv, a_sinks[j].astype(F32), bias_tab, seq)
            wo, bo = a_w_o[j].astype(BF16), row(a_b_o[j])
        else:
            w = b_w_qkv[j]
            w_all = jnp.concatenate([w[:, :q_cols] * q_scale, w[:, q_cols:]], axis=1).astype(BF16)
            qkv = _project(xt, w_all, jnp.zeros((1, w_all.shape[1]), F32))
            o = _sb_attention(qkv, batch, seq)
            wo, bo = b_w_o[j].astype(BF16), jnp.zeros((1, d), F32)
        wr = jnp.pad(moe_w_router[i], ((0, 0), (0, ROUTER_PAD - N_EXPERTS)))
        br = jnp.pad(moe_b_router[i], (0, ROUTER_PAD - N_EXPERTS)).reshape(1, -1)
        x1, logits = _post_attention(o, xt, wo, bo, row(ln1_g[i]), row(ln1_b[i]), wr, br)
        xt = _moe_layer(x1, logits[:, :N_EXPERTS],
                        moe_w_gate_up[i].astype(BF16), moe_b_gate_up[i].reshape(N_EXPERTS, 1, -1),
                        moe_w_down[i].astype(BF16), moe_b_down[i].reshape(N_EXPERTS, 1, -1),
                        row(ln2_g[i]), row(ln2_b[i]))
    return xt.reshape(batch, seq, d)
```

```python
import functools
import math

import jax
import jax.numpy as jnp
from jax import lax
from jax.experimental import pallas as pl
from jax.experimental.pallas import tpu as pltpu

D_MODEL = 1024
HEAD_DIM = 64
N_HEADS = 16
N_PAIRS = N_HEADS // 2
BLOCK = 128
A_KV_HEADS = 2
WINDOW = 128
N_BUCKETS = 32
MAX_DISTANCE = WINDOW
N_EXPERTS = 32
TOP_K = 4
D_FF = 1024
SWIGLU_LIMIT = 7.0
SWIGLU_ALPHA = 1.702
DEPTH = 4
DN_ALPHA = (2 * DEPTH) ** 0.25
LN_EPS = 1e-5

LANES = 128
MOE_ROWS = 512
ROW_TILE = 8
DMA_UNROLL = 8
GATHER_DMA_PRIORITY = 1
TOK_TILE = 256
PROJ_ROWS = 512
PROJ_COLS = 512
FF_CHUNK = 256
CAST_ROWS = 64
ROUTER_PAD = 128
MASK_VALUE = -1e30
SB_LOG_FLOOR = -88.0
SB_PAIRS = 2
VMEM_LIMIT = 56 * 1024 * 1024

F32 = jnp.float32
BF16 = jnp.bfloat16


def _params(n_grid):
    return pltpu.CompilerParams(dimension_semantics=("arbitrary",) * n_grid,
                                vmem_limit_bytes=VMEM_LIMIT)


def _proj_kernel(x_ref, w_ref, b_ref, o_ref):
    x = x_ref[...].astype(BF16)
    n_out = o_ref.shape[1]
    for c in range(n_out // PROJ_COLS):
        cols = slice(c * PROJ_COLS, (c + 1) * PROJ_COLS)
        acc = jnp.dot(x, w_ref[:, cols], preferred_element_type=F32)
        o_ref[:, cols] = (acc + b_ref[:, cols]).astype(o_ref.dtype)


def _project(x, w, b):
    n, d = x.shape
    n_out = w.shape[1]
    return pl.pallas_call(
        _proj_kernel,
        out_shape=jax.ShapeDtypeStruct((n, n_out), BF16),
        grid=(n // PROJ_ROWS,),
        in_specs=[pl.BlockSpec((PROJ_ROWS, d), lambda i: (i, 0)),
                  pl.BlockSpec((d, n_out), lambda i: (0, 0)),
                  pl.BlockSpec((1, n_out), lambda i: (0, 0))],
        out_specs=pl.BlockSpec((PROJ_ROWS, n_out), lambda i: (i, 0)),
        compiler_params=_params(1),
        name="proj",
    )(x, w, b)


def _swa_kernel(nb, sinks_ref, q_ref, kp_ref, kc_ref, vp_ref, vc_ref, bias_ref, o_ref):
    first = (pl.program_id(0) % nb) == 0
    lane = lax.broadcasted_iota(jnp.int32, (BLOCK, LANES), 1)
    even = lane < HEAD_DIM
    col = lax.broadcasted_iota(jnp.int32, (BLOCK, 2 * BLOCK), 1)
    key_ok = jnp.logical_or(col >= BLOCK, jnp.logical_not(first))
    for p in range(N_PAIRS):
        g = p // (N_PAIRS // A_KV_HEADS)
        gl = slice(g * LANES, (g + 1) * LANES)
        qp = q_ref[:, p * LANES:(p + 1) * LANES]
        kd = jnp.concatenate([kp_ref[:, gl], kc_ref[:, gl]], axis=0)
        vd = jnp.concatenate([vp_ref[:, gl], vc_ref[:, gl]], axis=0)
        zero = jnp.zeros_like(qp)
        outs = []
        for hh, qm in enumerate((jnp.where(even, qp, zero), jnp.where(even, zero, qp))):
            h = 2 * p + hh
            s = lax.dot_general(qm, kd, (((1,), (1,)), ((), ())),
                                preferred_element_type=F32)
            s = jnp.where(key_ok, s + bias_ref[h], MASK_VALUE)
            sink = sinks_ref[h]
            m = jnp.maximum(jnp.max(s, axis=-1, keepdims=True), sink)
            e = jnp.exp(s - m)
            den = jnp.sum(e, axis=-1, keepdims=True) + jnp.exp(sink - m)
            pr = (e / den).astype(BF16)
            outs.append(jnp.dot(pr, vd, preferred_element_type=F32))
        o_ref[:, p * LANES:(p + 1) * LANES] = jnp.where(even, outs[0], outs[1]).astype(o_ref.dtype)


def _swa_attention(qkv, sinks, bias_tab, seq):
    n = qkv.shape[0]
    nb = seq // BLOCK
    q_cols = N_HEADS * HEAD_DIM
    kv_cols = 2 * LANES
    k_blk = q_cols // kv_cols
    v_blk = k_blk + 1

    def prev(i):
        return jnp.maximum(i - 1, 0)

    return pl.pallas_call(
        functools.partial(_swa_kernel, nb),
        out_shape=jax.ShapeDtypeStruct((n, q_cols), BF16),
        grid=(n // BLOCK,),
        in_specs=[pl.BlockSpec(memory_space=pltpu.SMEM),
                  pl.BlockSpec((BLOCK, q_cols), lambda i: (i, 0)),
                  pl.BlockSpec((BLOCK, kv_cols), lambda i: (prev(i), k_blk)),
                  pl.BlockSpec((BLOCK, kv_cols), lambda i: (i, k_blk)),
                  pl.BlockSpec((BLOCK, kv_cols), lambda i: (prev(i), v_blk)),
                  pl.BlockSpec((BLOCK, kv_cols), lambda i: (i, v_blk)),
                  pl.BlockSpec((N_HEADS, BLOCK, 2 * BLOCK), lambda i: (0, 0, 0))],
        out_specs=pl.BlockSpec((BLOCK, q_cols), lambda i: (i, 0)),
        compiler_params=_params(1),
        name="swa_attention",
    )(sinks, qkv, qkv, qkv, qkv, qkv, bias_tab)


def _t5_bucket(n):
    max_exact = N_BUCKETS // 2
    nf = jnp.maximum(n, 1).astype(F32)
    large = max_exact + (jnp.log(nf / max_exact) / math.log(MAX_DISTANCE / max_exact)
                         * (N_BUCKETS - max_exact)).astype(jnp.int32)
    large = jnp.minimum(large, N_BUCKETS - 1)
    return jnp.where(n < max_exact, n, large)


def _swa_bias_table(rel_bias):
    qi = jnp.arange(BLOCK)[:, None]
    ki = jnp.arange(2 * BLOCK)[None, :]
    dist = qi + BLOCK - ki
    in_win = (dist >= 0) & (dist < WINDOW)
    bias = rel_bias[_t5_bucket(jnp.clip(dist, 0, None))]
    bias = bias.transpose(2, 0, 1).astype(F32)
    return jnp.where(in_win[None], bias, MASK_VALUE)


def _sb_kernel(q_ref, k_ref, v_ref, tri_ref, o_ref, carry_ref, acc_ref):
    i = pl.program_id(2)
    lane = lax.broadcasted_iota(jnp.int32, (BLOCK, LANES), 1)
    even = lane < HEAD_DIM
    tri_wide = tri_ref[...]
    tri_narrow = jnp.concatenate([tri_wide[:BLOCK, :BLOCK], tri_wide[:BLOCK, 2 * BLOCK:]], axis=1)
    start0 = pl.multiple_of(jnp.maximum(i - 1, 0) * BLOCK, BLOCK)
    row = lax.broadcasted_iota(jnp.int32, (2 * BLOCK, 2 * BLOCK), 0) % BLOCK
    col = lax.broadcasted_iota(jnp.int32, (2 * BLOCK, 2 * BLOCK), 1)
    causal = (start0 + col) < (i * BLOCK + row)

    def sweep(p, start, n_keys, first):
        cols = slice(p * LANES, (p + 1) * LANES)
        qp = q_ref[:, cols]
        zero = jnp.zeros_like(qp)
        q2 = jnp.concatenate([jnp.where(even, qp, zero), jnp.where(even, zero, qp)], axis=0)
        kb = k_ref[pl.ds(start, n_keys), cols]
        vb = v_ref[pl.ds(start, n_keys), cols]
        z = lax.dot_general(q2, kb, (((1,), (1,)), ((), ())), preferred_element_type=F32)
        sp = jnp.maximum(z, 0.0) + jnp.log1p(jnp.exp(-jnp.abs(z)))
        log_1m_beta = -sp
        if first:
            log_1m_beta = jnp.where(causal, log_1m_beta, 0.0)
        hi = log_1m_beta.astype(BF16)
        lo = (log_1m_beta - hi.astype(F32)).astype(BF16)
        tri = tri_wide if first else tri_narrow
        cs = (jnp.dot(hi, tri, preferred_element_type=F32)
              + jnp.dot(lo, tri, preferred_element_type=F32))
        log_w = (z - sp) + cs[:, :n_keys]
        if not first:
            log_w = log_w + carry_ref[p]
        w = jnp.exp(log_w)
        if first:
            w = jnp.where(causal, w, 0.0)
        pv = jnp.dot(w.astype(BF16), vb, preferred_element_type=F32)
        if first:
            acc_ref[p] = pv
            new_carry = cs[:, n_keys:]
        else:
            acc_ref[p] += pv
            new_carry = carry_ref[p] + cs[:, n_keys:]
        carry_ref[p] = new_carry
        return jnp.max(new_carry)

    def sweep_all(start, n_keys, first):
        tops = [sweep(p, start, n_keys, first) for p in range(SB_PAIRS)]
        return functools.reduce(jnp.maximum, tops)

    top = sweep_all(start0, 2 * BLOCK, True)

    def cond(state):
        j, top = state
        return jnp.logical_and(j >= 0, top > SB_LOG_FLOOR)

    def body(state):
        j, _ = state
        return j - 1, sweep_all(pl.multiple_of(j * BLOCK, BLOCK), BLOCK, False)

    lax.while_loop(cond, body, (i - 2, top))
    for p in range(SB_PAIRS):
        acc = acc_ref[p]
        o_ref[:, p * LANES:(p + 1) * LANES] = jnp.where(even, acc[:BLOCK], acc[BLOCK:]).astype(o_ref.dtype)


def _sb_attention(qkv, batch, seq):
    n = qkv.shape[0]
    nb = seq // BLOCK
    groups = N_PAIRS // SB_PAIRS
    width = SB_PAIRS * LANES
    tri = jnp.concatenate(
        [(jnp.arange(2 * BLOCK)[:, None] > jnp.arange(2 * BLOCK)[None, :]).astype(BF16),
         jnp.ones((2 * BLOCK, BLOCK), BF16)], axis=1)
    return pl.pallas_call(
        _sb_kernel,
        out_shape=jax.ShapeDtypeStruct((n, N_HEADS * HEAD_DIM), BF16),
        grid=(batch, groups, nb),
        in_specs=[pl.BlockSpec((BLOCK, width), lambda b, p, i: (b * nb + i, p)),
                  pl.BlockSpec((seq, width), lambda b, p, i: (b, groups + p)),
                  pl.BlockSpec((seq, width), lambda b, p, i: (b, 2 * groups + p)),
                  pl.BlockSpec((2 * BLOCK, 3 * BLOCK), lambda b, p, i: (0, 0))],
        out_specs=pl.BlockSpec((BLOCK, width), lambda b, p, i: (b * nb + i, p)),
        scratch_shapes=[pltpu.VMEM((SB_PAIRS, 2 * BLOCK, LANES), F32),
                        pltpu.VMEM((SB_PAIRS, 2 * BLOCK, LANES), F32)],
        compiler_params=_params(3),
        name="sb_attention",
    )(qkv, qkv, qkv, tri)


def _layer_norm(h, g, b):
    mu = jnp.mean(h, axis=-1, keepdims=True)
    xc = h - mu
    var = jnp.mean(xc * xc, axis=-1, keepdims=True)
    return xc * lax.rsqrt(var + LN_EPS) * g + b


def _to_row_tiles(ref, value):
    rows = value.shape[0]
    for s in range(ROW_TILE):
        ref[pl.ds(s, rows, stride=ROW_TILE), :] = value[:, s * LANES:(s + 1) * LANES]


def _from_row_tiles(ref, first, rows, stride):
    return jnp.concatenate(
        [ref[pl.ds(first + s, rows, stride=stride), :] for s in range(ROW_TILE)], axis=1)


def _post_attn_kernel(o_ref, x_ref, wo_ref, bo_ref, g_ref, b_ref, wr_ref, br_ref,
                      x1_ref, lg_ref):
    y = jnp.dot(o_ref[...], wo_ref[...], preferred_element_type=F32) + bo_ref[...]
    x1 = _layer_norm(DN_ALPHA * x_ref[...] + y, g_ref[...], b_ref[...])
    _to_row_tiles(x1_ref, x1)
    lg_ref[...] = jnp.dot(x1, wr_ref[...], preferred_element_type=F32,
                          precision=lax.Precision.HIGHEST) + br_ref[...]


def _post_attention(o, x, wo, bo, g, b, wr, br):
    n, d = x.shape
    row = lambda i: (i, 0)
    fixed = lambda i: (0, 0)
    return pl.pallas_call(
        _post_attn_kernel,
        out_shape=(jax.ShapeDtypeStruct((n * ROW_TILE, LANES), F32),
                   jax.ShapeDtypeStruct((n, ROUTER_PAD), F32)),
        grid=(n // TOK_TILE,),
        in_specs=[pl.BlockSpec((TOK_TILE, d), row),
                  pl.BlockSpec((TOK_TILE, d), row),
                  pl.BlockSpec((d, d), fixed),
                  pl.BlockSpec((1, d), fixed),
                  pl.BlockSpec((1, d), fixed),
                  pl.BlockSpec((1, d), fixed),
                  pl.BlockSpec((d, ROUTER_PAD), fixed),
                  pl.BlockSpec((1, ROUTER_PAD), fixed)],
        out_specs=(pl.BlockSpec((TOK_TILE * ROW_TILE, LANES), row),
                   pl.BlockSpec((TOK_TILE, ROUTER_PAD), row)),
        compiler_params=_params(1),
        name="post_attention",
    )(o, x, wo, bo, g, b, wr, br)


def _expert_kernel(be_ref, nused_ref, tok_ref, tok_next_ref, dst_ref, x1_ref,
                   wgu_ref, bgu_ref, wd_ref, bd_ref, y4_ref,
                   xbuf, ybuf, wgu_bf, wd_bf, gsem, ssem):
    i = pl.program_id(0)
    n_steps = pl.num_programs(0)
    cur = i % 2
    nxt = 1 - cur
    block_tiles = MOE_ROWS * ROW_TILE

    def row_copy(src, src_row, dst, dst_row, sem):
        return pltpu.make_async_copy(
            src.at[pl.ds(pl.multiple_of(src_row * ROW_TILE, ROW_TILE), ROW_TILE), :],
            dst.at[pl.ds(pl.multiple_of(dst_row * ROW_TILE, ROW_TILE), ROW_TILE), :], sem)

    def start_gather(idx_ref, buf_slot):
        def body(c, carry):
            for u in range(DMA_UNROLL):
                r = c * DMA_UNROLL + u
                row_copy(x1_ref, idx_ref[0, 0, r], xbuf.at[buf_slot], r,
                         gsem.at[buf_slot]).start(priority=GATHER_DMA_PRIORITY)
            return carry
        lax.fori_loop(0, MOE_ROWS // DMA_UNROLL, body, 0)

    def start_scatter(buf_slot):
        def body(c, carry):
            for u in range(DMA_UNROLL):
                r = c * DMA_UNROLL + u
                row_copy(ybuf.at[buf_slot], r, y4_ref, dst_ref[0, 0, r], ssem.at[buf_slot]).start()
            return carry
        lax.fori_loop(0, MOE_ROWS // DMA_UNROLL, body, 0)

    def wait_gather(buf_slot):
        pltpu.make_async_copy(x1_ref.at[pl.ds(0, block_tiles), :], xbuf.at[buf_slot],
                              gsem.at[buf_slot]).wait()

    def wait_scatter(buf_slot):
        pltpu.make_async_copy(ybuf.at[buf_slot], y4_ref.at[pl.ds(0, block_tiles), :],
                              ssem.at[buf_slot]).wait()

    @pl.when(i == 0)
    def _():
        ybuf[...] = jnp.zeros_like(ybuf)
        start_gather(tok_ref, 0)

    @pl.when(i + 1 < n_steps)
    def _():
        start_gather(tok_next_ref, nxt)

    new_expert = jnp.logical_or(i == 0, be_ref[i] != be_ref[jnp.maximum(i - 1, 0)])

    @pl.when(jnp.logical_and(new_expert, i < nused_ref[0]))
    def _():
        def cast_rows(c, carry):
            rows = pl.ds(pl.multiple_of(c * CAST_ROWS, CAST_ROWS), CAST_ROWS)
            wgu_bf[rows, :] = wgu_ref[rows, :].astype(BF16)
            wd_bf[rows, :] = wd_ref[rows, :].astype(BF16)
            return carry
        lax.fori_loop(0, D_MODEL // CAST_ROWS, cast_rows, 0)

    wait_gather(cur)

    @pl.when(i >= 2)
    def _():
        wait_scatter(cur)

    @pl.when(i < nused_ref[0])
    def _():
        x = _from_row_tiles(xbuf.at[cur], 0, MOE_ROWS, ROW_TILE).astype(BF16)
        y = jnp.zeros((MOE_ROWS, D_MODEL), F32)
        for c in range(D_FF // FF_CHUNK):
            gc = slice(c * FF_CHUNK, (c + 1) * FF_CHUNK)
            uc = slice(D_FF + c * FF_CHUNK, D_FF + (c + 1) * FF_CHUNK)
            g = jnp.dot(x, wgu_bf[:, gc], preferred_element_type=F32) + bgu_ref[:, gc]
            u = jnp.dot(x, wgu_bf[:, uc], preferred_element_type=F32) + bgu_ref[:, uc]
            g = jnp.minimum(g, SWIGLU_LIMIT)
            u = jnp.clip(u, -SWIGLU_LIMIT, SWIGLU_LIMIT)
            act = (u + 1.0) * (g * jax.nn.sigmoid(SWIGLU_ALPHA * g))
            y = y + jnp.dot(act.astype(BF16), wd_bf[gc, :], preferred_element_type=F32)
        _to_row_tiles(ybuf.at[cur], y + bd_ref[...])

    start_scatter(cur)

    @pl.when(i == n_steps - 1)
    def _():
        wait_scatter(nxt)
        wait_scatter(cur)


def _experts(block_e, n_used, slot_tok, slot_dst, x1, layer, wgu, bgu, wd, bd, n_out_rows):
    n_blocks = slot_tok.shape[0]
    assert n_blocks >= 2
    d = D_MODEL
    last = n_blocks - 1
    idx_block = (1, 1, MOE_ROWS)
    expert = lambda i, be, nu: (layer, be[i], 0, 0)
    grid_spec = pltpu.PrefetchScalarGridSpec(
        num_scalar_prefetch=2,
        grid=(n_blocks,),
        in_specs=[pl.BlockSpec(idx_block, lambda i, be, nu: (i, 0, 0), memory_space=pltpu.SMEM),
                  pl.BlockSpec(idx_block, lambda i, be, nu: (jnp.minimum(i + 1, last), 0, 0),
                               memory_space=pltpu.SMEM),
                  pl.BlockSpec(idx_block, lambda i, be, nu: (i, 0, 0), memory_space=pltpu.SMEM),
                  pl.BlockSpec(memory_space=pl.ANY),
                  pl.BlockSpec((None, None, d, 2 * D_FF), expert),
                  pl.BlockSpec((None, None, 1, 2 * D_FF), expert),
                  pl.BlockSpec((None, None, D_FF, d), expert),
                  pl.BlockSpec((None, None, 1, d), expert)],
        out_specs=pl.BlockSpec(memory_space=pl.ANY),
        scratch_shapes=[pltpu.VMEM((2, MOE_ROWS * ROW_TILE, LANES), F32),
                        pltpu.VMEM((2, MOE_ROWS * ROW_TILE, LANES), F32),
                        pltpu.VMEM((d, 2 * D_FF), BF16),
                        pltpu.VMEM((D_FF, d), BF16),
                        pltpu.SemaphoreType.DMA((2,)),
                        pltpu.SemaphoreType.DMA((2,))],
    )
    return pl.pallas_call(
        _expert_kernel,
        out_shape=jax.ShapeDtypeStruct((n_out_rows * ROW_TILE, LANES), F32),
        grid_spec=grid_spec,
        compiler_params=_params(1),
        name="moe_experts",
    )(block_e, n_used, slot_tok, slot_tok, slot_dst, x1, wgu, bgu, wd, bd)


def _combine_kernel(gate_ref, x_ref, g_ref, b_ref, *refs):
    y_refs, o_ref = refs[:TOP_K], refs[TOP_K]
    gates = gate_ref[...]
    y = None
    for k in range(TOP_K):
        yk = _from_row_tiles(y_refs[k], 0, TOK_TILE, ROW_TILE) * gates[:, k:k + 1]
        y = yk if y is None else y + yk
    x1 = _from_row_tiles(x_ref, 0, TOK_TILE, ROW_TILE)
    o_ref[...] = _layer_norm(DN_ALPHA * x1 + y, g_ref[...], b_ref[...])


def _combine(gates, x1, g, b, y4):
    n = gates.shape[0]
    d = D_MODEL
    tiles = n // TOK_TILE
    row = lambda i: (i, 0)
    fixed = lambda i: (0, 0)
    tile_rows = TOK_TILE * ROW_TILE
    y_specs = [pl.BlockSpec((tile_rows, LANES), functools.partial(lambda k, i: (k * tiles + i, 0), k))
               for k in range(TOP_K)]
    return pl.pallas_call(
        _combine_kernel,
        out_shape=jax.ShapeDtypeStruct((n, d), F32),
        grid=(tiles,),
        in_specs=[pl.BlockSpec((TOK_TILE, TOP_K), row),
                  pl.BlockSpec((tile_rows, LANES), row),
                  pl.BlockSpec((1, d), fixed),
                  pl.BlockSpec((1, d), fixed)] + y_specs,
        out_specs=pl.BlockSpec((TOK_TILE, d), row),
        compiler_params=_params(1),
        name="moe_combine",
    )(gates, x1, g, b, *([y4] * TOP_K))


def _route(logits):
    n = logits.shape[0]
    nk = n * TOP_K
    top_logits, top_idx = lax.top_k(logits, TOP_K)
    gates = jax.nn.softmax(top_logits, axis=-1)
    flat_e = top_idx.reshape(-1).astype(jnp.int32)
    order = jnp.sort(flat_e * nk + jnp.arange(nk, dtype=jnp.int32)) % nk
    experts = jnp.arange(N_EXPERTS, dtype=jnp.int32)
    counts = (flat_e[:, None] == experts[None, :]).sum(axis=0).astype(jnp.int32)
    padded = (counts + MOE_ROWS - 1) // MOE_ROWS * MOE_ROWS
    pad_end = jnp.cumsum(padded)
    pad_start = pad_end - padded
    start = jnp.cumsum(counts) - counts
    n_blocks = nk // MOE_ROWS + N_EXPERTS
    blk = jnp.arange(n_blocks, dtype=jnp.int32)[:, None]
    block_e = jnp.minimum((pad_end[None, :] <= blk * MOE_ROWS).sum(axis=1),
                          N_EXPERTS - 1).astype(jnp.int32)
    n_used = (pad_end[-1] // MOE_ROWS).reshape(1).astype(jnp.int32)
    r = jnp.arange(MOE_ROWS, dtype=jnp.int32)[None, :]
    e = block_e[:, None]
    idx = blk * MOE_ROWS + r - pad_start[e]
    valid = idx < counts[e]
    src = order[jnp.clip(start[e] + idx, 0, nk - 1)]
    slot_tok = jnp.where(valid, src // TOP_K, 0).astype(jnp.int32)
    slot_dst = jnp.where(valid, (src % TOP_K) * n + src // TOP_K,
                         nk + (blk % 2) * MOE_ROWS + r).astype(jnp.int32)
    return gates, slot_tok[:, None, :], slot_dst[:, None, :], block_e, n_used


def _moe_layer(x1, logits, layer, wgu, bgu, wd, bd, g, b):
    n = logits.shape[0]
    gates, slot_tok, slot_dst, block_e, n_used = _route(logits)
    y4 = _experts(block_e, n_used, slot_tok, slot_dst, x1, layer, wgu, bgu, wd, bd,
                  n * TOP_K + 2 * MOE_ROWS)
    return _combine(gates, x1, g, b, y4)


def kernel(x, rel_bias, a_w_qkv, a_b_qkv, a_sinks, a_w_o, a_b_o, b_w_qkv, b_w_o, ln1_g, ln1_b, ln2_g, ln2_b, moe_w_router, moe_b_router, moe_w_gate_up, moe_b_gate_up, moe_w_down, moe_b_down):
    batch, seq, d = x.shape
    n = batch * seq
    q_cols = N_HEADS * HEAD_DIM
    q_scale = HEAD_DIM ** -0.5
    xt = x.reshape(n, d)
    bias_tab = _swa_bias_table(rel_bias)
    row = lambda v: v.reshape(1, -1).astype(F32)
    b_gate_up = moe_b_gate_up.reshape(DEPTH, N_EXPERTS, 1, -1)
    b_down = moe_b_down.reshape(DEPTH, N_EXPERTS, 1, -1)

    def dup_heads(t):
        parts = [t[..., h * HEAD_DIM:(h + 1) * HEAD_DIM] for h in range(A_KV_HEADS)]
        return jnp.concatenate([parts[0], parts[0], parts[1], parts[1]], axis=-1)

    for i in range(DEPTH):
        j = i // 2
        if i % 2 == 0:
            w, bq = a_w_qkv[j], a_b_qkv[j]
            kv = A_KV_HEADS * HEAD_DIM
            w_all = jnp.concatenate([w[:, :q_cols] * q_scale,
                                     dup_heads(w[:, q_cols:q_cols + kv]),
                                     dup_heads(w[:, q_cols + kv:])], axis=1).astype(BF16)
            b_all = jnp.concatenate([bq[:q_cols] * q_scale,
                                     dup_heads(bq[q_cols:q_cols + kv]),
                                     dup_heads(bq[q_cols + kv:])]).reshape(1, -1)
            qkv = _project(xt, w_all, b_all)
            o = _swa_attention(qkv, a_sinks[j].astype(F32), bias_tab, seq)
            wo, bo = a_w_o[j].astype(BF16), row(a_b_o[j])
        else:
            w = b_w_qkv[j]
            w_all = jnp.concatenate([w[:, :q_cols] * q_scale, w[:, q_cols:]], axis=1).astype(BF16)
            qkv = _project(xt, w_all, jnp.zeros((1, w_all.shape[1]), F32))
            o = _sb_attention(qkv, batch, seq)
            wo, bo = b_w_o[j].astype(BF16), jnp.zeros((1, d), F32)
        wr = jnp.pad(moe_w_router[i], ((0, 0), (0, ROUTER_PAD - N_EXPERTS)))
        br = jnp.pad(moe_b_router[i], (0, ROUTER_PAD - N_EXPERTS)).reshape(1, -1)
        x1, logits = _post_attention(o, xt, wo, bo, row(ln1_g[i]), row(ln1_b[i]), wr, br)
        xt = _moe_layer(x1, logits[:, :N_EXPERTS], i, moe_w_gate_up, b_gate_up, moe_w_down, b_down,
                        row(ln2_g[i]), row(ln2_b[i]))
    return xt.reshape(batch, seq, d)
```
